```python
import jax, jax.numpy as jnp
from jax import lax
import numpy as np

D_MODEL = 1024
BATCH = 16
SEQ = 256
DEPTH = 1
DEC_BATCH = 8
DEC_SEQ = 1024
PAST_LEN = 256

GRID_W = 64
HEAD_DIM = 64
RET_HEADS = 8
RET_DK = 64
RET_DV = 64
RET_QK_WIDTH = RET_HEADS * RET_DK
RET_WIDTH = RET_HEADS * RET_DV
NA_HEADS = 8
NA_WIDTH = NA_HEADS * HEAD_DIM
MIX_WIDTH = RET_WIDTH + NA_WIDTH
IN_WIDTH = 2 * RET_QK_WIDTH + 2 * RET_WIDTH + 3 * NA_WIDTH
CHUNK = 128
WIN_H = 8
WIN_W = 16
QBLOCK = 128
ROPE_BASE = 10000.0
PEER_HEADS = 8
PEER_DKEY = 256
N_KEYS = 128
N_EXPERTS = N_KEYS * N_KEYS
PEER_TOPK = 16
TOKEN_BLOCK = 128
EPS = 1e-6
NEG_INF = -1e30

kernel_name = "hymba_retnet_natten_peer_prefix_dit"


def rms_norm(x, g):
    xf = x.astype(jnp.float32)
    y = xf * lax.rsqrt(jnp.mean(xf * xf, axis=-1, keepdims=True) + EPS)
    return (y * g.astype(jnp.float32)).astype(x.dtype)


def modulate(x, g, shift, scale):
    return rms_norm(x, g) * (1 + scale) + shift


def adaln(cvec, w_ada, b_ada):
    m = jax.nn.silu(cvec) @ w_ada + b_ada
    return jnp.split(m, 6, axis=-1)


def heads(x, n_heads):
    B, T, _ = x.shape
    return x.reshape(B, T, n_heads, -1)


def project_in(h, w_in):
    proj = h @ w_in
    cuts = np.cumsum([RET_QK_WIDTH, RET_QK_WIDTH, RET_WIDTH, RET_WIDTH, NA_WIDTH, NA_WIDTH]).tolist()
    return jnp.split(proj, cuts, axis=-1)


def rope_2d(x):
    T = x.shape[1]
    t = jnp.arange(T)
    rows = (t // GRID_W).astype(jnp.float32)
    cols = (t % GRID_W).astype(jnp.float32)
    half = RET_DK // 2
    n_freq = half // 2
    freqs = ROPE_BASE ** (-jnp.arange(n_freq, dtype=jnp.float32) / n_freq)

    def rot(xh, pos):
        ang = pos[:, None] * freqs[None, :]
        cos = jnp.cos(ang)[None, :, None, :]
        sin = jnp.sin(ang)[None, :, None, :]
        x1, x2 = xh[..., :n_freq], xh[..., n_freq:]
        return jnp.concatenate([x1 * cos - x2 * sin, x1 * sin + x2 * cos], axis=-1)

    out = jnp.concatenate([rot(x[..., :half], rows), rot(x[..., half:], cols)], axis=-1)
    return out.astype(x.dtype)


def retention_scan(q, k, v, decay_logit, s0, strict):
    B, T, H, dk = q.shape
    dv = v.shape[-1]
    n = T // CHUNK
    lg = jax.nn.log_sigmoid(decay_logit.astype(jnp.float32))
    pos = jnp.arange(CHUNK, dtype=jnp.float32)
    diff = pos[:, None] - pos[None, :]
    mask = (diff > 0) if strict else (diff >= 0)
    dmat = jnp.where(mask[None], jnp.exp(lg[:, None, None] * jnp.maximum(diff, 0.0)[None]), 0.0)
    xi = jnp.exp(lg[:, None] * (pos + 1.0)[None])
    zeta = jnp.exp(lg[:, None] * (CHUNK - 1.0 - pos)[None])
    cdec = jnp.exp(lg * CHUNK)

    def chunks(a):
        return a.astype(jnp.float32).reshape(B, n, CHUNK, H, a.shape[-1]).transpose(1, 0, 3, 2, 4)

    qs, ks, vs = chunks(q), chunks(k) * (dk ** -0.5), chunks(v)

    def step(S, inp):
        qc, kc, vc = inp
        sc = jnp.einsum('bhnd,bhmd->bhnm', qc, kc) * dmat[None]
        inner = jnp.einsum('bhnm,bhme->bhne', sc, vc)
        cross = jnp.einsum('bhnd,bhde->bhne', qc, S) * xi[None, :, :, None]
        S_new = S * cdec[None, :, None, None] + jnp.einsum('bhmd,bhme->bhde', kc * zeta[None, :, :, None], vc)
        return S_new, inner + cross

    s_fin, out = lax.scan(step, s0.astype(jnp.float32), (qs, ks, vs))
    out = out.transpose(1, 0, 3, 2, 4).reshape(B, T, H, dv)
    return out, s_fin


def bidir_retention(q, k, v, logit_f, logit_b, s0_f, s0_b):
    y_f, s_f = retention_scan(q, k, v, logit_f, s0_f, False)
    y_b, s_b = retention_scan(q[:, ::-1], k[:, ::-1], v[:, ::-1], logit_b, s0_b, True)
    return y_f + y_b[:, ::-1], s_f, s_b


def retention_output(y, g, gn_g):
    B, T = y.shape[:2]
    yn = y * lax.rsqrt(jnp.mean(y * y, axis=-1, keepdims=True) + EPS)
    yn = yn * gn_g.astype(jnp.float32).reshape(RET_HEADS, RET_DV)
    return (jax.nn.silu(g.astype(jnp.float32)) * yn.reshape(B, T, RET_WIDTH)).astype(g.dtype)


def dense_context_attention(q, k, v):
    B, T, H, d = q.shape
    qb = q.reshape(B, T // QBLOCK, QBLOCK, H, d).transpose(1, 0, 2, 3, 4)

    def blk(qi):
        s = jnp.einsum('bqhd,bkhd->bhqk', qi, k).astype(jnp.float32) * (d ** -0.5)
        p = jax.nn.softmax(s, axis=-1).astype(v.dtype)
        return jnp.einsum('bhqk,bkhd->bqhd', p, v)

    o = lax.map(blk, qb)
    return o.transpose(1, 0, 2, 3, 4).reshape(B, T, H, d)


def neighbourhood_attention(q, k, v, k_ctx, v_ctx, rpb):
    B, T, H, d = q.shape
    rows = T // GRID_W
    kh = min(WIN_H, rows)
    n_win = kh * GRID_W
    qg = q.reshape(B, rows, GRID_W, H, d)
    kg = k.reshape(B, rows, GRID_W, H, d)
    vg = v.reshape(B, rows, GRID_W, H, d)
    kcol = jnp.tile(jnp.arange(GRID_W), kh)
    krow_off = jnp.repeat(jnp.arange(kh), GRID_W)
    qcol = jnp.arange(GRID_W)
    cstart = jnp.clip(qcol - WIN_W // 2, 0, GRID_W - WIN_W)
    in_win = (kcol[None, :] >= cstart[:, None]) & (kcol[None, :] < cstart[:, None] + WIN_W)
    dc_idx = jnp.clip(kcol[None, :] - qcol[:, None] + WIN_W - 1, 0, 2 * WIN_W - 2)
    scale = d ** -0.5

    def row(r):
        rs = jnp.clip(r - kh // 2, 0, rows - kh)
        q_r = lax.dynamic_index_in_dim(qg, r, axis=1, keepdims=False)
        k_r = lax.dynamic_slice_in_dim(kg, rs, kh, axis=1).reshape(B, n_win, H, d)
        v_r = lax.dynamic_slice_in_dim(vg, rs, kh, axis=1).reshape(B, n_win, H, d)
        dr_idx = rs + krow_off - r + WIN_H - 1
        bias = rpb[:, dr_idx[None, :], dc_idx].astype(jnp.float32)
        s_win = jnp.einsum('bqhd,bkhd->bhqk', q_r, k_r).astype(jnp.float32) * scale + bias[None]
        s_win = jnp.where(in_win[None, None], s_win, NEG_INF)
        s_ctx = jnp.einsum('bqhd,bkhd->bhqk', q_r, k_ctx).astype(jnp.float32) * scale
        p = jax.nn.softmax(jnp.concatenate([s_win, s_ctx], axis=-1), axis=-1).astype(v.dtype)
        return (jnp.einsum('bhqk,bkhd->bqhd', p[..., :n_win], v_r)
                + jnp.einsum('bhqk,bkhd->bqhd', p[..., n_win:], v_ctx))

    o = lax.map(row, jnp.arange(rows))
    return o.transpose(1, 0, 2, 3, 4).reshape(B, T, H, d)


def peer(x, wq, keys, u, v):
    B, T, D = x.shape
    xt = x.reshape(-1, TOKEN_BLOCK, D)
    half = PEER_DKEY // 2

    def blk(xb):
        q = (xb @ wq).reshape(TOKEN_BLOCK, PEER_HEADS, PEER_DKEY)
        s1 = jnp.einsum('nhd,hkd->nhk', q[..., :half], keys[:, 0]).astype(jnp.float32)
        s2 = jnp.einsum('nhd,hkd->nhk', q[..., half:], keys[:, 1]).astype(jnp.float32)
        v1, i1 = lax.top_k(s1, PEER_TOPK)
        v2, i2 = lax.top_k(s2, PEER_TOPK)
        cand = (v1[..., :, None] + v2[..., None, :]).reshape(TOKEN_BLOCK, PEER_HEADS, PEER_TOPK * PEER_TOPK)
        cidx = (i1[..., :, None] * N_KEYS + i2[..., None, :]).reshape(TOKEN_BLOCK, PEER_HEADS, PEER_TOPK * PEER_TOPK)
        top, sel = lax.top_k(cand, PEER_TOPK)
        eidx = jnp.take_along_axis(cidx, sel, axis=-1)
        gate = jax.nn.softmax(top, axis=-1).astype(x.dtype)
        act = jax.nn.gelu(jnp.einsum('nd,nhkd->nhk', xb, u[eidx]))
        return jnp.einsum('nhk,nhkd->nd', gate * act, v[eidx])

    return lax.map(blk, xt).reshape(B, T, D)


def context_layer(x, c_ctx, w_ada, b_ada, norm1_g, norm2_g, w_in, ret_decay_f, ret_decay_b,
                  ret_gn_g, na_qn_g, na_kn_g, w_out, peer_wq, peer_keys, peer_u, peer_v):
    B, T, _ = x.shape
    sh1, sc1, g1, sh2, sc2, g2 = adaln(c_ctx, w_ada, b_ada)
    h = modulate(x, norm1_g, sh1, sc1)
    rq, rk, rv, rg, nq, nk, nv = project_in(h, w_in)
    zeros = jnp.zeros((B, RET_HEADS, RET_DK, RET_DV), jnp.float32)
    y_r, s_f, s_b = bidir_retention(heads(rq, RET_HEADS), heads(rk, RET_HEADS), heads(rv, RET_HEADS),
                                    ret_decay_f, ret_decay_b, zeros, zeros)
    o_r = retention_output(y_r, rg, ret_gn_g)
    qn = rms_norm(heads(nq, NA_HEADS), na_qn_g)
    kn = rms_norm(heads(nk, NA_HEADS), na_kn_g)
    vn = heads(nv, NA_HEADS)
    o_n = dense_context_attention(qn, kn, vn).reshape(B, T, NA_WIDTH)
    x = x + g1 * (jnp.concatenate([o_r, o_n], axis=-1) @ w_out)
    x = x + g2 * peer(modulate(x, norm2_g, sh2, sc2), peer_wq, peer_keys, peer_u, peer_v)
    state = jnp.stack([s_f, s_b], axis=1)
    return x, kn, vn, state


def latent_layer(x, c, k_ctx, v_ctx, state, w_ada, b_ada, norm1_g, norm2_g, w_in, ret_decay_f,
                 ret_decay_b, ret_gn_g, na_qn_g, na_kn_g, na_rpb, w_out, peer_wq, peer_keys, peer_u, peer_v):
    B, T, _ = x.shape
    sh1, sc1, g1, sh2, sc2, g2 = [m[:, None, :] for m in adaln(c, w_ada, b_ada)]
    h = modulate(x, norm1_g, sh1, sc1)
    rq, rk, rv, rg, nq, nk, nv = project_in(h, w_in)
    y_r, _, _ = bidir_retention(rope_2d(heads(rq, RET_HEADS)), rope_2d(heads(rk, RET_HEADS)),
                                heads(rv, RET_HEADS), ret_decay_f, ret_decay_b,
                                state[:, 0], state[:, 1])
    o_r = retention_output(y_r, rg, ret_gn_g)
    qn = rms_norm(heads(nq, NA_HEADS), na_qn_g)
    kn = rms_norm(heads(nk, NA_HEADS), na_kn_g)
    vn = heads(nv, NA_HEADS)
    o_n = neighbourhood_attention(qn, kn, vn, k_ctx, v_ctx, na_rpb).reshape(B, T, NA_WIDTH)
    x = x + g1 * (jnp.concatenate([o_r, o_n], axis=-1) @ w_out)
    x = x + g2 * peer(modulate(x, norm2_g, sh2, sc2), peer_wq, peer_keys, peer_u, peer_v)
    return x


def setup_inputs(seed: int = 0) -> dict:
    key = jax.random.key(seed)
    ks = jax.random.split(key, 24)
    f32 = jnp.float32
    nrm = lambda k, shape, s: jax.random.normal(k, shape, f32) * s
    base_logit = np.log(2.0 ** (5.0 + np.arange(RET_HEADS)) - 1.0).astype(np.float32)
    base_logit = jnp.asarray(base_logit)[None, :]
    return {
        "x_prompt": nrm(ks[0], (BATCH, SEQ, D_MODEL), 1.0),
        "x_sample": nrm(ks[1], (DEC_BATCH, DEC_SEQ, D_MODEL), 1.0),
        "cache_na_k": nrm(ks[2], (DEC_BATCH, DEPTH, PAST_LEN, NA_HEADS, HEAD_DIM), 1.0),
        "cache_na_v": nrm(ks[3], (DEC_BATCH, DEPTH, PAST_LEN, NA_HEADS, HEAD_DIM), 1.0),
        "state_ret": nrm(ks[4], (DEC_BATCH, DEPTH, 2, RET_HEADS, RET_DK, RET_DV), 0.5),
        "c": nrm(ks[5], (DEC_BATCH, D_MODEL), 1.0),
        "c_ctx": nrm(ks[6], (D_MODEL,), 1.0),
        "w_ada": nrm(ks[7], (DEPTH, D_MODEL, 6 * D_MODEL), 0.5 * D_MODEL ** -0.5),
        "b_ada": nrm(ks[8], (DEPTH, 6 * D_MODEL), 0.01),
        "norm1_g": 1.0 + nrm(ks[9], (DEPTH, D_MODEL), 0.02),
        "norm2_g": 1.0 + nrm(ks[10], (DEPTH, D_MODEL), 0.02),
        "w_in": nrm(ks[11], (DEPTH, D_MODEL, IN_WIDTH), D_MODEL ** -0.5),
        "ret_decay_f": base_logit + nrm(ks[12], (DEPTH, RET_HEADS), 0.1),
        "ret_decay_b": base_logit + nrm(ks[13], (DEPTH, RET_HEADS), 0.1),
        "ret_gn_g": 1.0 + nrm(ks[14], (DEPTH, RET_WIDTH), 0.02),
        "na_qn_g": 1.0 + nrm(ks[15], (DEPTH, HEAD_DIM), 0.02),
        "na_kn_g": 1.0 + nrm(ks[16], (DEPTH, HEAD_DIM), 0.02),
        "na_rpb": nrm(ks[17], (DEPTH, NA_HEADS, 2 * WIN_H - 1, 2 * WIN_W - 1), 0.1),
        "w_out": nrm(ks[18], (DEPTH, MIX_WIDTH, D_MODEL), MIX_WIDTH ** -0.5),
        "peer_wq": nrm(ks[19], (DEPTH, D_MODEL, PEER_HEADS * PEER_DKEY), D_MODEL ** -0.5),
        "peer_keys": nrm(ks[20], (DEPTH, PEER_HEADS, 2, N_KEYS, PEER_DKEY // 2), (PEER_DKEY // 2) ** -0.5),
        "peer_u": nrm(ks[21], (DEPTH, N_EXPERTS, D_MODEL), D_MODEL ** -0.5),
        "peer_v": nrm(ks[22], (DEPTH, N_EXPERTS, D_MODEL), (PEER_HEADS * PEER_TOPK) ** -0.5),
    }


def reference(x_prompt, x_sample, cache_na_k, cache_na_v, state_ret, c, c_ctx, w_ada, b_ada,
              norm1_g, norm2_g, w_in, ret_decay_f, ret_decay_b, ret_gn_g, na_qn_g, na_kn_g,
              na_rpb, w_out, peer_wq, peer_keys, peer_u, peer_v):
    y_p = x_prompt
    y_s = x_sample
    new_k, new_v, new_s = [], [], []
    for l in range(DEPTH):
        y_p, k_l, v_l, s_l = context_layer(
            y_p, c_ctx, w_ada[l], b_ada[l], norm1_g[l], norm2_g[l], w_in[l], ret_decay_f[l],
            ret_decay_b[l], ret_gn_g[l], na_qn_g[l], na_kn_g[l], w_out[l], peer_wq[l],
            peer_keys[l], peer_u[l], peer_v[l])
        new_k.append(k_l)
        new_v.append(v_l)
        new_s.append(s_l)
        y_s = latent_layer(
            y_s, c, cache_na_k[:, l], cache_na_v[:, l], state_ret[:, l], w_ada[l], b_ada[l],
            norm1_g[l], norm2_g[l], w_in[l], ret_decay_f[l], ret_decay_b[l], ret_gn_g[l],
            na_qn_g[l], na_kn_g[l], na_rpb[l], w_out[l], peer_wq[l], peer_keys[l], peer_u[l], peer_v[l])
    new_na_k = jnp.stack(new_k, axis=1)
    new_na_v = jnp.stack(new_v, axis=1)
    new_state_ret = jnp.stack(new_s, axis=1)
    return (y_p, y_s, new_na_k, new_na_v, new_state_ret)
```

```python
import functools

import jax
import jax.numpy as jnp
from jax import lax
from jax.experimental import pallas as pl
from jax.experimental.pallas import tpu as pltpu

F32 = jnp.float32
BF16 = jnp.bfloat16

D_MODEL = 1024
N_HEADS = 8
HEAD_DIM = 64
GROUP_WIDTH = N_HEADS * HEAD_DIM
N_PIECES = 7
GRID_W = 64
WIN_H = 8
WIN_W = 16
ROPE_BASE = 10000.0
PEER_HEADS = 8
PEER_HALF = 128
N_KEYS = 128
PEER_TOPK = 16
EPS = 1e-6
NEG_INF = -1e30
QK_SCALE = HEAD_DIM ** -0.5

VMEM_LIMIT = 56 * 1024 * 1024


def _params(n_axes):
    return pltpu.CompilerParams(dimension_semantics=("arbitrary",) * n_axes,
                                vmem_limit_bytes=VMEM_LIMIT)


def _dot(a, b):
    return jnp.dot(a, b, preferred_element_type=F32)


def _dot_nt(a, b):
    return lax.dot_general(a, b, (((1,), (1,)), ((), ())), preferred_element_type=F32)


def _dot_tn(a, b):
    return lax.dot_general(a, b, (((0,), (0,)), ((), ())), preferred_element_type=F32)


def _silu(x):
    return x * jax.nn.sigmoid(x)


def _ada_kernel(c_ref, w_ref, b_ref, o_ref):
    s = _silu(c_ref[...])
    o_ref[...] = jnp.dot(s, w_ref[...], preferred_element_type=F32,
                         precision=lax.Precision.HIGHEST) + b_ref[...]


def _ada(cc, w_ada, b_ada):
    rows, d = cc.shape
    n = w_ada.shape[1]
    tn = 1536
    return pl.pallas_call(
        _ada_kernel,
        grid=(n // tn,),
        in_specs=[pl.BlockSpec((rows, d), lambda j: (0, 0)),
                  pl.BlockSpec((d, tn), lambda j: (0, j)),
                  pl.BlockSpec((1, tn), lambda j: (0, j))],
        out_specs=pl.BlockSpec((rows, tn), lambda j: (0, j)),
        out_shape=jax.ShapeDtypeStruct((rows, n), F32),
        compiler_params=_params(1),
        name="ada",
    )(cc, w_ada, b_ada.reshape(1, n))


def _modulated_norm(x, gain, shift, scale):
    y = x * lax.rsqrt(jnp.mean(x * x, axis=-1, keepdims=True) + EPS)
    return (y * gain) * (1.0 + scale) + shift


def _head_rms(x, gain, bd):
    x2 = x * x
    hi = x2.astype(BF16)
    lo = (x2 - hi.astype(F32)).astype(BF16)
    ms = _dot(hi, bd) + _dot(lo, bd)
    return x * lax.rsqrt(ms + EPS) * gain


def _swap_halves(x):
    n = x.shape[-1]
    lane = lax.broadcasted_iota(jnp.int32, x.shape, x.ndim - 1)
    up = pltpu.roll(x, n - 16, x.ndim - 1)
    down = pltpu.roll(x, 16, x.ndim - 1)
    return jnp.where((lane % 32) < 16, up, down)


def _in_kernel(*refs, rope):
    if rope:
        (x_ref, mod_ref, g_ref, w_ref, qg_ref, kg_ref, bd_ref, cos_ref, sin_ref,
         rq_ref, rk_ref, rv_ref, rg_ref, nq_ref, nk_ref, nv_ref) = refs
    else:
        (x_ref, mod_ref, g_ref, w_ref, qg_ref, kg_ref, bd_ref,
         rq_ref, rk_ref, rv_ref, rg_ref, nq_ref, nk_ref, nv_ref) = refs
    h = _modulated_norm(x_ref[...], g_ref[...], mod_ref[0:1, :], mod_ref[1:2, :]).astype(BF16)

    def piece(p):
        return _dot(h, w_ref[:, p * GROUP_WIDTH:(p + 1) * GROUP_WIDTH])

    rq, rk = piece(0), piece(1)
    if rope:
        cos, sin = cos_ref[...], sin_ref[...]
        rq = rq * cos + _swap_halves(rq) * sin
        rk = rk * cos + _swap_halves(rk) * sin
    rq_ref[...] = rq
    rk_ref[...] = rk
    rv_ref[...] = piece(2)
    rg_ref[...] = piece(3)
    bd = bd_ref[...]
    nq_ref[...] = _head_rms(piece(4), qg_ref[...], bd)
    nk_ref[...] = _head_rms(piece(5), kg_ref[...], bd)
    nv_ref[...] = piece(6)


def _inproj(x2d, mod3, mod_row, seq, norm_g, w_in, qg, kg, bd, rope_tables):
    rows, d = x2d.shape
    tm = 256
    per_seq = seq // tm
    row_spec = pl.BlockSpec((tm, d), lambda i: (i, 0))
    out_spec = pl.BlockSpec((tm, GROUP_WIDTH), lambda i: (i, 0))
    vec512 = pl.BlockSpec((1, GROUP_WIDTH), lambda i: (0, 0))
    in_specs = [row_spec,
                pl.BlockSpec((None, 6, d), lambda i: (mod_row(i // per_seq), 0, 0)),
                pl.BlockSpec((1, d), lambda i: (0, 0)),
                pl.BlockSpec(w_in.shape, lambda i: (0, 0)),
                vec512, vec512,
                pl.BlockSpec((GROUP_WIDTH, GROUP_WIDTH), lambda i: (0, 0))]
    args = [x2d, mod3, norm_g, w_in, qg, kg, bd]
    if rope_tables is not None:
        tab_spec = pl.BlockSpec((tm, GROUP_WIDTH), lambda i: (i % per_seq, 0))
        in_specs += [tab_spec, tab_spec]
        args += list(rope_tables)
    return pl.pallas_call(
        functools.partial(_in_kernel, rope=rope_tables is not None),
        grid=(rows // tm,),
        in_specs=in_specs,
        out_specs=[out_spec] * N_PIECES,
        out_shape=[jax.ShapeDtypeStruct((rows, GROUP_WIDTH), F32)] * N_PIECES,
        compiler_params=_params(1),
        name="inproj",
    )(*args)


def _ret_kernel(*refs, seq, tq, latent):
    if latent:
        q_ref, k_ref, v_ref, g_ref, dec_ref, gn_ref, s0_ref, o_ref = refs
    else:
        q_ref, k_ref, v_ref, g_ref, dec_ref, gn_ref, o_ref, st_ref = refs
    lg = jax.nn.log_sigmoid(dec_ref[...])
    kf = k_ref[...] * QK_SCALE
    kb = kf.astype(BF16)
    vb = v_ref[...].astype(BF16)
    gn = gn_ref[...]
    spos = lax.broadcasted_iota(jnp.int32, (1, seq), 1).astype(F32)

    def q_tile(i, carry):
        r0 = pl.multiple_of(i * tq, tq)
        qt = q_ref[pl.ds(r0, tq), :].astype(BF16)
        gt = g_ref[pl.ds(r0, tq), :]
        tpos = (r0 + lax.broadcasted_iota(jnp.int32, (tq, 1), 0)).astype(F32)
        diff = tpos - spos
        fwd = diff >= 0.0
        adiff = jnp.abs(diff)
        for h in range(N_HEADS):
            sl = slice(h * HEAD_DIM, (h + 1) * HEAD_DIM)
            lgf = lg[0:1, h * HEAD_DIM:h * HEAD_DIM + 1]
            lgb = lg[1:2, h * HEAD_DIM:h * HEAD_DIM + 1]
            decay = jnp.exp(jnp.where(fwd, lgf, lgb) * adiff)
            qh = qt[:, sl]
            s = _dot_nt(qh, kb[:, sl]) * decay
            y = _dot(s.astype(BF16), vb[:, sl])
            if latent:
                y = y + _dot(qh, s0_ref[0, h].astype(BF16)) * jnp.exp(lgf * (tpos + 1.0))
                y = y + _dot(qh, s0_ref[1, h].astype(BF16)) * jnp.exp(lgb * (seq - tpos))
            yn = y * lax.rsqrt(jnp.mean(y * y, axis=-1, keepdims=True) + EPS) * gn[:, sl]
            o_ref[pl.ds(r0, tq), sl] = (_silu(gt[:, sl]) * yn).astype(o_ref.dtype)
        return carry

    lax.fori_loop(0, seq // tq, q_tile, 0)

    if not latent:
        scol = lax.broadcasted_iota(jnp.int32, (seq, 1), 0).astype(F32)
        zf = jnp.exp(lg[0:1, :] * (seq - 1.0 - scol))
        zb = jnp.exp(lg[1:2, :] * scol)
        sf = _dot_tn((kf * zf).astype(BF16), vb)
        sb = _dot_tn((kf * zb).astype(BF16), vb)
        for h in range(N_HEADS):
            sl = slice(h * HEAD_DIM, (h + 1) * HEAD_DIM)
            st_ref[0, h] = sf[sl, sl]
            st_ref[1, h] = sb[sl, sl]


def _retention(rq, rk, rv, rg, dec, gn, s0):
    b, seq, w = rq.shape
    latent = s0 is not None
    tq = 256
    seq_spec = pl.BlockSpec((None, seq, w), lambda i: (i, 0, 0))
    st_spec = pl.BlockSpec((None, 2, N_HEADS, HEAD_DIM, HEAD_DIM), lambda i: (i, 0, 0, 0, 0))
    in_specs = [seq_spec] * 4 + [pl.BlockSpec((2, w), lambda i: (0, 0)),
                                 pl.BlockSpec((1, w), lambda i: (0, 0))]
    args = [rq, rk, rv, rg, dec, gn]
    o_shape = jax.ShapeDtypeStruct((b, seq, w), BF16)
    if latent:
        in_specs.append(st_spec)
        args.append(s0)
        out_specs, out_shape = seq_spec, o_shape
    else:
        out_specs = [seq_spec, st_spec]
        out_shape = [o_shape, jax.ShapeDtypeStruct((b, 2, N_HEADS, HEAD_DIM, HEAD_DIM), F32)]
    return pl.pallas_call(
        functools.partial(_ret_kernel, seq=seq, tq=tq, latent=latent),
        grid=(b,),
        in_specs=in_specs,
        out_specs=out_specs,
        out_shape=out_shape,
        compiler_params=_params(1),
        name="retention",
    )(*args)


def _softmax_rows(parts):
    m = functools.reduce(jnp.maximum, [jnp.max(p, axis=-1, keepdims=True) for p in parts])
    es = [jnp.exp(p - m) for p in parts]
    inv = 1.0 / functools.reduce(jnp.add, [jnp.sum(e, axis=-1, keepdims=True) for e in es])
    return [(e * inv).astype(BF16) for e in es]


def _cattn_kernel(q_ref, k_ref, v_ref, o_ref):
    qb = q_ref[...].astype(BF16)
    kb = k_ref[...].astype(BF16)
    vb = v_ref[...].astype(BF16)
    for h in range(N_HEADS):
        sl = slice(h * HEAD_DIM, (h + 1) * HEAD_DIM)
        (p,) = _softmax_rows([_dot_nt(qb[:, sl], kb[:, sl]) * QK_SCALE])
        o_ref[:, sl] = _dot(p, vb[:, sl]).astype(o_ref.dtype)


def _context_attention(qn, kn, vn):
    b, seq, w = qn.shape
    spec = pl.BlockSpec((None, seq, w), lambda i: (i, 0, 0))
    return pl.pallas_call(
        _cattn_kernel,
        grid=(b,),
        in_specs=[spec] * 3,
        out_specs=spec,
        out_shape=jax.ShapeDtypeStruct((b, seq, w), BF16),
        compiler_params=_params(1),
        name="context_attention",
    )(qn, kn, vn)


def _bias_kernel(rpb_ref, o_ref, *, rows):
    h = pl.program_id(0)
    kh = min(WIN_H, rows)
    n_dr, n_dc = 2 * WIN_H - 1, 2 * WIN_W - 1
    qc = lax.broadcasted_iota(jnp.int32, (GRID_W, GRID_W), 0)
    kc = lax.broadcasted_iota(jnp.int32, (GRID_W, GRID_W), 1)
    cstart = jnp.clip(qc - WIN_W // 2, 0, GRID_W - WIN_W)
    in_win = (kc >= cstart) & (kc < cstart + WIN_W)
    dc = kc - qc + WIN_W - 1
    neg = jnp.full((GRID_W, GRID_W), NEG_INF, F32)
    col_blocks = []
    for a in range(n_dr):
        t = neg
        for c in range(n_dc):
            t = jnp.where(in_win & (dc == c), rpb_ref[(h * n_dr + a) * n_dc + c], t)
        col_blocks.append(t)
    for qr in range(rows):
        rs = min(max(qr - kh // 2, 0), rows - kh)
        for kr in range(rows):
            blk = col_blocks[kr - qr + WIN_H - 1] if rs <= kr < rs + kh else neg
            o_ref[qr * GRID_W:(qr + 1) * GRID_W, kr * GRID_W:(kr + 1) * GRID_W] = blk


def _na_bias(rpb, seq):
    n_heads = rpb.shape[0]
    return pl.pallas_call(
        functools.partial(_bias_kernel, rows=seq // GRID_W),
        grid=(n_heads,),
        in_specs=[pl.BlockSpec(memory_space=pltpu.SMEM)],
        out_specs=pl.BlockSpec((None, seq, seq), lambda h: (h, 0, 0)),
        out_shape=jax.ShapeDtypeStruct((n_heads, seq, seq), F32),
        compiler_params=_params(1),
        name="na_bias",
    )(rpb.reshape(-1))


def _na_kernel(q_ref, k_ref, v_ref, kc_ref, vc_ref, bias_ref, o_ref, *, seq, tq):
    kb = k_ref[...].astype(BF16)
    vb = v_ref[...].astype(BF16)
    kcb = kc_ref[...].astype(BF16)
    vcb = vc_ref[...].astype(BF16)

    def q_tile(i, carry):
        r0 = pl.multiple_of(i * tq, tq)
        qt = q_ref[pl.ds(r0, tq), :].astype(BF16)
        for hh in range(2):
            sl = slice(hh * HEAD_DIM, (hh + 1) * HEAD_DIM)
            qh = qt[:, sl]
            s_win = _dot_nt(qh, kb[:, sl]) * QK_SCALE + bias_ref[hh, pl.ds(r0, tq), :]
            s_ctx = _dot_nt(qh, kcb[:, sl]) * QK_SCALE
            p_win, p_ctx = _softmax_rows([s_win, s_ctx])
            o = _dot(p_win, vb[:, sl]) + _dot(p_ctx, vcb[:, sl])
            o_ref[pl.ds(r0, tq), sl] = o.astype(o_ref.dtype)
        return carry

    lax.fori_loop(0, seq // tq, q_tile, 0)


def _neighbourhood_attention(qn, kn, vn, k_ctx, v_ctx, bias):
    b, seq, w = qn.shape
    past = k_ctx.shape[1]
    pair = 2 * HEAD_DIM
    tq = 256
    lat_spec = pl.BlockSpec((None, seq, pair), lambda hp, i: (i, 0, hp))
    ctx_spec = pl.BlockSpec((None, past, pair), lambda hp, i: (i, 0, hp))
    return pl.pallas_call(
        functools.partial(_na_kernel, seq=seq, tq=tq),
        grid=(w // pair, b),
        in_specs=[lat_spec] * 3 + [ctx_spec] * 2
        + [pl.BlockSpec((2, seq, seq), lambda hp, i: (hp, 0, 0))],
        out_specs=lat_spec,
        out_shape=jax.ShapeDtypeStruct((b, seq, w), BF16),
        compiler_params=_params(2),
        name="neighbourhood_attention",
    )(qn, kn, vn, k_ctx, v_ctx, bias)


def _out_kernel(x_ref, or_ref, on_ref, w_ref, mod_ref, g_ref, x1_ref, h2_ref):
    half = w_ref.shape[0] // 2
    a = _dot(or_ref[...], w_ref[0:half, :]) + _dot(on_ref[...], w_ref[half:, :])
    x1 = x_ref[...] + mod_ref[2:3, :] * a
    x1_ref[...] = x1
    h2_ref[...] = _modulated_norm(x1, g_ref[...], mod_ref[3:4, :], mod_ref[4:5, :]).astype(BF16)


def _outproj(x2d, o_r, o_n, w_out, mod3, mod_row, seq, norm_g):
    rows, d = x2d.shape
    tm = 256
    per_seq = seq // tm
    row_spec = pl.BlockSpec((tm, d), lambda i: (i, 0))
    half_spec = pl.BlockSpec((tm, GROUP_WIDTH), lambda i: (i, 0))
    return pl.pallas_call(
        _out_kernel,
        grid=(rows // tm,),
        in_specs=[row_spec, half_spec, half_spec,
                  pl.BlockSpec(w_out.shape, lambda i: (0, 0)),
                  pl.BlockSpec((None, 6, d), lambda i: (mod_row(i // per_seq), 0, 0)),
                  pl.BlockSpec((1, d), lambda i: (0, 0))],
        out_specs=[row_spec, row_spec],
        out_shape=[jax.ShapeDtypeStruct((rows, d), F32), jax.ShapeDtypeStruct((rows, d), BF16)],
        compiler_params=_params(1),
        name="outproj",
    )(x2d, o_r, o_n, w_out, mod3, norm_g)


def _top_values(s, n, out_ref):
    for r in range(n):
        m = jnp.max(s, axis=0, keepdims=True)
        out_ref[r:r + 1, :] = m
        s = jnp.where(s == m, NEG_INF, s)


def _route_kernel(h_ref, wq_ref, keys_ref, thr_ref, coef_ref, fac_ref, s2_ref,
                  v1_ref, v2_ref, cand_ref, top_ref):
    q = _dot(h_ref[...], wq_ref[...]).astype(BF16)
    k = PEER_TOPK
    for h in range(PEER_HEADS):
        base = h * 2 * PEER_HALF
        s1 = _dot_nt(keys_ref[h, 0], q[:, base:base + PEER_HALF])
        s2 = _dot_nt(keys_ref[h, 1], q[:, base + PEER_HALF:base + 2 * PEER_HALF])
        _top_values(s1, k + 1, v1_ref)
        _top_values(s2, k + 1, v2_ref)
        cand_ref[0:k, :] = v1_ref[0:1, :] + v2_ref[0:k, :]
        for a in range(1, 8):
            cand_ref[k + 8 * (a - 1):k + 8 * a, :] = v1_ref[a:a + 1, :] + v2_ref[0:8, :]
        cand_ref[72:80, :] = v1_ref[8:k, :] + v2_ref[0:1, :]
        cand_ref[80:88, :] = jnp.full((8, cand_ref.shape[1]), NEG_INF, F32)
        cand_ref[80:81, :] = v1_ref[k:k + 1, :] + v2_ref[0:1, :]
        cand_ref[81:82, :] = v1_ref[0:1, :] + v2_ref[k:k + 1, :]
        _top_values(cand_ref[...], k + 1, top_ref)
        z = jnp.sum(jnp.exp(top_ref[0:k, :] - top_ref[0:1, :]), axis=0, keepdims=True)
        tau = 0.5 * (top_ref[k - 1:k, :] + top_ref[k:k + 1, :])
        thr_ref[h] = tau - s1
        coef_ref[h] = jnp.exp(s1 - v1_ref[0:1, :]) / z
        fac_ref[h] = jnp.exp(s2 - v2_ref[0:1, :])
        s2_ref[h] = s2


def _route(h2, wq, keys):
    n, d = h2.shape
    tn = 256
    out_spec = pl.BlockSpec((PEER_HEADS, N_KEYS, tn), lambda t: (0, 0, t))
    out_shape = jax.ShapeDtypeStruct((PEER_HEADS, N_KEYS, n), F32)
    return pl.pallas_call(
        _route_kernel,
        grid=(n // tn,),
        in_specs=[pl.BlockSpec((tn, d), lambda t: (t, 0)),
                  pl.BlockSpec(wq.shape, lambda t: (0, 0)),
                  pl.BlockSpec(keys.shape, lambda t: (0, 0, 0, 0))],
        out_specs=[out_spec] * 4,
        out_shape=[out_shape] * 4,
        scratch_shapes=[pltpu.VMEM((24, tn), F32), pltpu.VMEM((24, tn), F32),
                        pltpu.VMEM((88, tn), F32), pltpu.VMEM((24, tn), F32)],
        compiler_params=_params(1),
        name="peer_route",
    )(h2, wq, keys)


def _peer_kernel(h_ref, u_ref, vt_ref, thr_ref, coef_ref, fac_ref, s2_ref, x1_ref, mod_ref,
                 y_ref, acc_ref, p_ref, *, tc):
    e = pl.program_id(1)
    te, tn = p_ref.shape
    i_per_tile = te // N_KEYS

    @pl.when(e == 0)
    def _():
        acc_ref[...] = jnp.zeros_like(acc_ref)

    for c in range(tn // tc):
        cols = slice(c * tc, (c + 1) * tc)
        act = jax.nn.gelu(_dot_nt(u_ref[...], h_ref[cols, :]))
        for il in range(i_per_tile):
            rows = slice(il * N_KEYS, (il + 1) * N_KEYS)
            w = jnp.zeros((N_KEYS, tc), F32)
            for h in range(PEER_HEADS):
                hit = s2_ref[h, :, cols] > thr_ref[h, il:il + 1, cols]
                w = w + jnp.where(hit, fac_ref[h, :, cols], 0.0) * coef_ref[h, il:il + 1, cols]
            p_ref[rows, cols] = (w * act[rows, :]).astype(BF16)
    acc_ref[...] += _dot(vt_ref[...], p_ref[...])

    @pl.when(e == pl.num_programs(1) - 1)
    def _():
        y_ref[...] = x1_ref[...] + mod_ref[5:6, :] * acc_ref[...].T


def _peer(h2, u, vt, thr, coef, fac, s2, x1, mod3, mod_row, seq):
    n, d = h2.shape
    n_exp = u.shape[0]
    tn, te, tc = 512, 1024, 256
    tn = min(tn, seq)
    per_seq = seq // tn
    tok_spec = pl.BlockSpec((tn, d), lambda t, e: (t, 0))
    tile_spec = pl.BlockSpec((PEER_HEADS, te // N_KEYS, tn), lambda t, e: (0, e, t))
    full_spec = pl.BlockSpec((PEER_HEADS, N_KEYS, tn), lambda t, e: (0, 0, t))
    return pl.pallas_call(
        functools.partial(_peer_kernel, tc=tc),
        grid=(n // tn, n_exp // te),
        in_specs=[tok_spec,
                  pl.BlockSpec((te, d), lambda t, e: (e, 0)),
                  pl.BlockSpec((d, te), lambda t, e: (0, e)),
                  tile_spec, tile_spec, full_spec, full_spec,
                  tok_spec,
                  pl.BlockSpec((None, 6, d), lambda t, e: (mod_row(t // per_seq), 0, 0))],
        out_specs=tok_spec,
        out_shape=jax.ShapeDtypeStruct((n, d), F32),
        scratch_shapes=[pltpu.VMEM((d, tn), F32), pltpu.VMEM((te, tn), BF16)],
        compiler_params=_params(2),
        name="peer_experts",
    )(h2, u, vt, thr, coef, fac, s2, x1, mod3)


def _rope_tables(seq):
    t = jnp.arange(seq)
    n_freq = HEAD_DIM // 4
    freqs = ROPE_BASE ** (-jnp.arange(n_freq, dtype=F32) / n_freq)
    ang_r = (t // GRID_W).astype(F32)[:, None] * freqs[None, :]
    ang_c = (t % GRID_W).astype(F32)[:, None] * freqs[None, :]
    cos = jnp.concatenate([jnp.cos(ang_r)] * 2 + [jnp.cos(ang_c)] * 2, axis=-1)
    sin = jnp.concatenate([-jnp.sin(ang_r), jnp.sin(ang_r), -jnp.sin(ang_c), jnp.sin(ang_c)], axis=-1)
    return jnp.tile(cos, (1, N_HEADS)), jnp.tile(sin, (1, N_HEADS))


def _sub_layer(x, mod3, latent_extras, weights):
    (norm1_g, norm2_g, w_in, dec, gn, qg, kg, bd, w_out, wq, keys, u, vt) = weights
    b, seq, d = x.shape
    x2d = x.reshape(b * seq, d)
    latent = latent_extras is not None
    mod_row = (lambda b_idx: 1 + b_idx) if latent else (lambda b_idx: 0)
    tables = _rope_tables(seq) if latent else None
    rq, rk, rv, rg, nq, nk, nv = _inproj(x2d, mod3, mod_row, seq, norm1_g, w_in, qg, kg, bd, tables)
    seqs = lambda a: a.reshape(b, seq, GROUP_WIDTH)
    if latent:
        k_ctx, v_ctx, s0, bias = latent_extras
        o_r = _retention(seqs(rq), seqs(rk), seqs(rv), seqs(rg), dec, gn, s0)
        o_n = _neighbourhood_attention(seqs(nq), seqs(nk), seqs(nv), k_ctx, v_ctx, bias)
        state = None
    else:
        o_r, state = _retention(seqs(rq), seqs(rk), seqs(rv), seqs(rg), dec, gn, None)
        o_n = _context_attention(seqs(nq), seqs(nk), seqs(nv))
    x1, h2 = _outproj(x2d, o_r.reshape(b * seq, GROUP_WIDTH), o_n.reshape(b * seq, GROUP_WIDTH),
                      w_out, mod3, mod_row, seq, norm2_g)
    thr, coef, fac, s2 = _route(h2, wq, keys)
    y = _peer(h2, u, vt, thr, coef, fac, s2, x1, mod3, mod_row, seq)
    return y.reshape(b, seq, d), nk, nv, state


def kernel(x_prompt, x_sample, cache_na_k, cache_na_v, state_ret, c, c_ctx, w_ada, b_ada, norm1_g, norm2_g, w_in, ret_decay_f, ret_decay_b, ret_gn_g, na_qn_g, na_kn_g, na_rpb, w_out, peer_wq, peer_keys, peer_u, peer_v):
    depth = w_ada.shape[0]
    batch, seq, d = x_prompt.shape
    dec_batch, dec_seq, _ = x_sample.shape
    past = cache_na_k.shape[2]
    n_mod = -(-(1 + dec_batch) // 8) * 8
    cc = jnp.zeros((n_mod, d), F32).at[0].set(c_ctx).at[1:1 + dec_batch].set(c)
    bd = jnp.kron(jnp.eye(N_HEADS, dtype=F32), jnp.full((HEAD_DIM, HEAD_DIM), 1.0 / HEAD_DIM, F32)).astype(BF16)

    y_p, y_s = x_prompt, x_sample
    new_k, new_v, new_s = [], [], []
    for l in range(depth):
        mod3 = _ada(cc, w_ada[l], b_ada[l]).reshape(n_mod, 6, d)
        weights = (norm1_g[l].reshape(1, d), norm2_g[l].reshape(1, d), w_in[l].astype(BF16),
                   jnp.stack([jnp.repeat(ret_decay_f[l], HEAD_DIM), jnp.repeat(ret_decay_b[l], HEAD_DIM)]),
                   ret_gn_g[l].reshape(1, GROUP_WIDTH),
                   jnp.tile(na_qn_g[l], N_HEADS).reshape(1, GROUP_WIDTH),
                   jnp.tile(na_kn_g[l], N_HEADS).reshape(1, GROUP_WIDTH),
                   bd, w_out[l].astype(BF16), peer_wq[l].astype(BF16), peer_keys[l].astype(BF16),
                   peer_u[l].astype(BF16), peer_v[l].astype(BF16).T)
        y_p, k_l, v_l, s_l = _sub_layer(y_p, mod3, None, weights)
        new_k.append(k_l.reshape(batch, seq, N_HEADS, HEAD_DIM))
        new_v.append(v_l.reshape(batch, seq, N_HEADS, HEAD_DIM))
        new_s.append(s_l)
        bias = _na_bias(na_rpb[l], dec_seq)
        extras = (cache_na_k[:, l].reshape(dec_batch, past, GROUP_WIDTH),
                  cache_na_v[:, l].reshape(dec_batch, past, GROUP_WIDTH),
                  state_ret[:, l], bias)
        y_s, _, _, _ = _sub_layer(y_s, mod3, extras, weights)
    return (y_p, y_s, jnp.stack(new_k, axis=1), jnp.stack(new_v, axis=1), jnp.stack(new_s, axis=1))
```
